```python
import math
import jax, jax.numpy as jnp
from jax import lax
import numpy as np

D_MODEL = 1024
BATCH = 8
SEQ = 2048
DEPTH = 1

CHUNK = 64
Q_BLOCK = 128
HEAD_DIM = 64
SB_HEADS = 8
SW_HEADS = 8
SW_KV_HEADS = 2
SW_GROUP = SW_HEADS // SW_KV_HEADS
SB_WIDTH = SB_HEADS * HEAD_DIM
SW_WIDTH = SW_HEADS * HEAD_DIM
SW_KV_WIDTH = SW_KV_HEADS * HEAD_DIM
MIX_WIDTH = SB_WIDTH + SW_WIDTH
IN_WIDTH = 3 * SB_WIDTH + SW_WIDTH + 2 * SW_KV_WIDTH
IN_SPLITS = [SB_WIDTH, 2 * SB_WIDTH, 3 * SB_WIDTH, 3 * SB_WIDTH + SW_WIDTH,
             3 * SB_WIDTH + SW_WIDTH + SW_KV_WIDTH]
WINDOW = 128
WINDOW_CHUNKS = WINDOW // CHUNK
KV_SPAN = 2 * Q_BLOCK
N_BUCKETS = 32
MAX_DISTANCE = 128
N_EXPERTS = 32
TOP_K = 4
D_FF = 1024
SWIGLU_LIMIT = 7.0
SWIGLU_ALPHA = 1.702
MOE_BLOCK = 128
RMS_EPS = 1e-5

kernel_name = "hybrid_stickbreak_swa_sinks_moe_block"


def rms_norm(x, g):
    xf = x.astype(jnp.float32)
    y = xf * lax.rsqrt(jnp.mean(xf * xf, axis=-1, keepdims=True) + RMS_EPS)
    return (y * g.astype(jnp.float32)).astype(x.dtype)


def t5_bucket(rel):
    nb = N_BUCKETS // 2
    max_exact = nb // 2
    bucket = jnp.where(rel > 0, nb, 0)
    n = jnp.abs(rel)
    nf = jnp.maximum(n, 1).astype(jnp.float32)
    large = max_exact + (jnp.log(nf / max_exact) / math.log(MAX_DISTANCE / max_exact)
                         * (nb - max_exact)).astype(jnp.int32)
    large = jnp.minimum(large, nb - 1)
    return bucket + jnp.where(n < max_exact, n, large)


def stick_breaking_attention(q, k, v):
    S = q.shape[1]
    scale = HEAD_DIM ** -0.5
    outs = []
    for blk in range(S // Q_BLOCK):
        start = blk * Q_BLOCK
        end = start + Q_BLOCK
        kb, vb = k[:, :end], v[:, :end]
        z = jnp.einsum('bqhd,bkhd->bhqk', q[:, start:end], kb,
                       preferred_element_type=jnp.float32) * scale
        q_pos = start + jnp.arange(Q_BLOCK)
        k_pos = jnp.arange(end)
        strict = k_pos[None, :] < q_pos[:, None]
        log_1m = jnp.where(strict, jax.nn.log_sigmoid(-z), 0.0)
        between = lax.cumsum(log_1m, axis=3, reverse=True) - log_1m
        w = jnp.where(strict, jnp.exp(jax.nn.log_sigmoid(z) + between), 0.0)
        outs.append(jnp.einsum('bhqk,bkhd->bqhd', w.astype(v.dtype), vb))
    return jnp.concatenate(outs, axis=1)


def sliding_window_attention(q, k, v, rel_bias, sinks):
    B, S = q.shape[:2]
    nb = S // Q_BLOCK
    scale = HEAD_DIM ** -0.5
    qb = q.reshape(B, nb, Q_BLOCK, SW_KV_HEADS, SW_GROUP, HEAD_DIM)

    def band(t):
        tp = jnp.pad(t, ((0, 0), (Q_BLOCK, 0), (0, 0), (0, 0)))
        tp = tp.reshape(B, nb + 1, Q_BLOCK, SW_KV_HEADS, HEAD_DIM)
        return jnp.concatenate([tp[:, :-1], tp[:, 1:]], axis=2)

    kb, vb = band(k), band(v)
    s = jnp.einsum('bnqhgd,bnkhd->bnhgqk', qb, kb,
                   preferred_element_type=jnp.float32) * scale
    s = s + rel_bias.reshape(SW_KV_HEADS, SW_GROUP, Q_BLOCK, KV_SPAN)
    qi = jnp.arange(Q_BLOCK)
    kj = jnp.arange(KV_SPAN) - Q_BLOCK
    dchunk = qi[:, None] // CHUNK - kj[None, :] // CHUNK
    band_mask = (dchunk >= 0) & (dchunk <= WINDOW_CHUNKS)
    k_valid = (jnp.arange(nb)[:, None] * Q_BLOCK + kj[None, :]) >= 0
    mask = band_mask[None] & k_valid[:, None, :]
    s = jnp.where(mask[None, :, None, None], s, -jnp.inf)
    sink = jnp.broadcast_to(sinks.astype(jnp.float32).reshape(1, 1, SW_KV_HEADS, SW_GROUP, 1, 1),
                            s.shape[:-1] + (1,))
    p = jax.nn.softmax(jnp.concatenate([s, sink], axis=-1), axis=-1)[..., :-1]
    o = jnp.einsum('bnhgqk,bnkhd->bnqhgd', p.astype(v.dtype), vb)
    return o.reshape(B, S, SW_WIDTH)


def moe_ffn(h, w_router, b_router, w_gate_up, b_gate_up, w_down, b_down):
    B, S, D = h.shape
    T = B * S
    xt = h.reshape(T, D)
    logits = jnp.matmul(xt, w_router, preferred_element_type=jnp.float32) + b_router.astype(jnp.float32)
    top_val, top_idx = lax.top_k(logits, TOP_K)
    gates = jax.nn.softmax(top_val, axis=-1)
    A = T * TOP_K
    flat_e = top_idx.reshape(A)
    order = jnp.argsort(flat_e)
    sorted_e = flat_e[order]
    counts = jnp.bincount(flat_e, length=N_EXPERTS)
    padded = (counts + MOE_BLOCK - 1) // MOE_BLOCK * MOE_BLOCK
    start = jnp.cumsum(counts) - counts
    pend = jnp.cumsum(padded)
    pstart = pend - padded
    dest = pstart[sorted_e] + jnp.arange(A) - start[sorted_e]
    P = A + N_EXPERTS * MOE_BLOCK
    n_blocks = P // MOE_BLOCK
    xs = jnp.zeros((P, D), h.dtype).at[dest].set(xt[order // TOP_K])
    block_e = jnp.searchsorted(pend, jnp.arange(n_blocks) * MOE_BLOCK, side='right')
    block_e = jnp.minimum(block_e, N_EXPERTS - 1).astype(jnp.int32)

    def expert_block(args):
        xb, e = args
        gu = xb @ w_gate_up[e] + b_gate_up[e]
        gate = jnp.minimum(gu[:, :D_FF], SWIGLU_LIMIT)
        up = jnp.clip(gu[:, D_FF:], -SWIGLU_LIMIT, SWIGLU_LIMIT)
        act = (up + 1.0) * (gate * jax.nn.sigmoid(SWIGLU_ALPHA * gate))
        return act @ w_down[e] + b_down[e]

    ys = lax.map(expert_block, (xs.reshape(n_blocks, MOE_BLOCK, D), block_e)).reshape(P, D)
    y_assign = jnp.zeros((A, D), h.dtype).at[order].set(ys[dest])
    out = jnp.einsum('tkd,tk->td', y_assign.reshape(T, TOP_K, D), gates.astype(h.dtype))
    return out.reshape(B, S, D)


def setup_inputs(seed: int = 0) -> dict:
    key = jax.random.key(seed)
    ks = jax.random.split(key, 20)
    f32 = jnp.float32
    n = lambda k, shape, s: (jax.random.normal(k, shape, f32) * s)
    return {
        "x": n(ks[0], (BATCH, SEQ, D_MODEL), 1.0),
        "attn_norm": 1.0 + n(ks[1], (DEPTH, D_MODEL), 0.05),
        "w_in": n(ks[2], (DEPTH, D_MODEL, IN_WIDTH), D_MODEL ** -0.5),
        "b_in": n(ks[3], (DEPTH, IN_WIDTH), 0.02),
        "sb_out_norm": 1.0 + n(ks[4], (DEPTH, SB_WIDTH), 0.05),
        "sw_out_norm": 1.0 + n(ks[5], (DEPTH, SW_WIDTH), 0.05),
        "sinks": n(ks[6], (DEPTH, SW_HEADS), 0.5),
        "rel_bias_table": n(ks[7], (N_BUCKETS, SW_HEADS), 0.5),
        "w_out": n(ks[8], (DEPTH, MIX_WIDTH, D_MODEL), MIX_WIDTH ** -0.5),
        "b_out": n(ks[9], (DEPTH, D_MODEL), 0.02),
        "ffn_norm": 1.0 + n(ks[10], (DEPTH, D_MODEL), 0.05),
        "w_router": n(ks[11], (DEPTH, D_MODEL, N_EXPERTS), D_MODEL ** -0.5),
        "b_router": n(ks[12], (DEPTH, N_EXPERTS), 0.01),
        "w_gate_up": n(ks[13], (DEPTH, N_EXPERTS, D_MODEL, 2 * D_FF), D_MODEL ** -0.5),
        "b_gate_up": n(ks[14], (DEPTH, N_EXPERTS, 2 * D_FF), 0.02),
        "w_down": n(ks[15], (DEPTH, N_EXPERTS, D_FF, D_MODEL), D_FF ** -0.5),
        "b_down": n(ks[16], (DEPTH, N_EXPERTS, D_MODEL), 0.02),
        "final_norm": 1.0 + n(ks[17], (D_MODEL,), 0.05),
    }


def reference(x, attn_norm, w_in, b_in, sb_out_norm, sw_out_norm, sinks, rel_bias_table,
              w_out, b_out, ffn_norm, w_router, b_router, w_gate_up, b_gate_up, w_down, b_down,
              final_norm):
    B, S, _ = x.shape
    qi = jnp.arange(Q_BLOCK)
    kj = jnp.arange(KV_SPAN) - Q_BLOCK
    buckets = t5_bucket(kj[None, :] - qi[:, None])
    rel_bias = jnp.transpose(rel_bias_table.astype(jnp.float32)[buckets], (2, 0, 1))
    for l in range(DEPTH):
        h = rms_norm(x, attn_norm[l])
        proj = h @ w_in[l] + b_in[l]
        sb_q, sb_k, sb_v, sw_q, sw_k, sw_v = jnp.split(proj, IN_SPLITS, axis=-1)
        sb = stick_breaking_attention(sb_q.reshape(B, S, SB_HEADS, HEAD_DIM),
                                      sb_k.reshape(B, S, SB_HEADS, HEAD_DIM),
                                      sb_v.reshape(B, S, SB_HEADS, HEAD_DIM)).reshape(B, S, SB_WIDTH)
        sw = sliding_window_attention(sw_q.reshape(B, S, SW_HEADS, HEAD_DIM),
                                      sw_k.reshape(B, S, SW_KV_HEADS, HEAD_DIM),
                                      sw_v.reshape(B, S, SW_KV_HEADS, HEAD_DIM),
                                      rel_bias, sinks[l])
        mixed = jnp.concatenate([rms_norm(sb, sb_out_norm[l]), rms_norm(sw, sw_out_norm[l])], axis=-1)
        x = x + mixed @ w_out[l] + b_out[l]
        x = x + moe_ffn(rms_norm(x, ffn_norm[l]), w_router[l], b_router[l],
                        w_gate_up[l], b_gate_up[l], w_down[l], b_down[l])
    return rms_norm(x, final_norm)
```

```python
import functools
import math

import jax
import jax.numpy as jnp
from jax import lax
from jax.experimental import pallas as pl
from jax.experimental.pallas import tpu as pltpu

D_MODEL = 1024
HEAD_DIM = 64
SB_HEADS = 8
SW_HEADS = 8
SW_KV_HEADS = 2
SB_WIDTH = SB_HEADS * HEAD_DIM
SW_WIDTH = SW_HEADS * HEAD_DIM
SW_KV_WIDTH = SW_KV_HEADS * HEAD_DIM
Q_BLOCK = 128
KV_SPAN = 2 * Q_BLOCK
CHUNK = 64
WINDOW_CHUNKS = 2
N_BUCKETS = 32
MAX_DISTANCE = 128
N_EXPERTS = 32
TOP_K = 4
D_FF = 1024
SWIGLU_LIMIT = 7.0
SWIGLU_ALPHA = 1.702
RMS_EPS = 1e-5

LANES = 128
HALF = D_MODEL // 2
PAIRS = SB_HEADS // 2

IN_TM = 512
SB_TQ = 256
SB_TK = 128
POST_TM = 512
EXP_BLK = 256
FF_CHUNK = 512
DISP_TM = 256
COMB_TM = 128
VMEM_LIMIT = 56 * 1024 * 1024

F32 = jnp.float32
BF16 = jnp.bfloat16
NEG_INF = float("-inf")


def _nt_dot(a, b):
    return lax.dot_general(a, b, (((1,), (1,)), ((), ())), preferred_element_type=F32)


def _rms(x, g):
    ms = jnp.mean(x * x, axis=-1, keepdims=True)
    return x * lax.rsqrt(ms + RMS_EPS) * g


def _pack_halves(lo, hi):
    lo_b = lax.bitcast_convert_type(lo.astype(BF16).astype(F32), jnp.uint32)
    hi_b = lax.bitcast_convert_type(hi.astype(BF16).astype(F32), jnp.uint32)
    return (lo_b >> 16) | hi_b


def _unpack_halves(p):
    lo = lax.bitcast_convert_type(p << 16, F32)
    hi = lax.bitcast_convert_type(p & jnp.uint32(0xFFFF0000), F32)
    return lo, hi


def _inproj_kernel(x_ref, g_ref, w_ref, b_ref, *out_refs):
    hb = _rms(x_ref[...], g_ref[...]).astype(BF16)
    c0 = 0
    for ref in out_refs:
        n = ref.shape[-1]
        acc = jnp.dot(hb, w_ref[:, c0:c0 + n], preferred_element_type=F32)
        ref[...] = (acc + b_ref[:, c0:c0 + n]).astype(BF16)
        c0 += n


def _inproj(x2, g, w, b, widths):
    t = x2.shape[0]
    n_all = sum(widths)
    return pl.pallas_call(
        _inproj_kernel,
        grid=(t // IN_TM,),
        in_specs=[
            pl.BlockSpec((IN_TM, D_MODEL), lambda i: (i, 0)),
            pl.BlockSpec((1, D_MODEL), lambda i: (0, 0)),
            pl.BlockSpec((D_MODEL, n_all), lambda i: (0, 0)),
            pl.BlockSpec((1, n_all), lambda i: (0, 0)),
        ],
        out_specs=[pl.BlockSpec((IN_TM, n), lambda i: (i, 0)) for n in widths],
        out_shape=[jax.ShapeDtypeStruct((t, n), BF16) for n in widths],
        compiler_params=pltpu.CompilerParams(
            dimension_semantics=("arbitrary",), vmem_limit_bytes=VMEM_LIMIT),
        name="inproj",
    )(x2, g, w, b)


def _sb_kernel(q_ref, k_ref, v_ref, g_ref, o_ref, qs_ref, acc_ref, carry_ref):
    qi = pl.program_id(1)
    tq, tk = SB_TQ, SB_TK
    nd = tq // tk

    low_q = lax.broadcasted_iota(jnp.int32, (tq, LANES), 1) < HEAD_DIM
    low_k = lax.broadcasted_iota(jnp.int32, (tk, LANES), 1) < HEAD_DIM
    r = lax.broadcasted_iota(jnp.int32, (tk, 2 * tk), 0)
    c = lax.broadcasted_iota(jnp.int32, (tk, 2 * tk), 1)
    tri = jnp.where((r > c) | (c >= tk), 1.0, 0.0).astype(BF16)

    zero_q = jnp.zeros((tq, LANES), BF16)
    for p in range(PAIRS):
        qp = q_ref[0, :, p * LANES:(p + 1) * LANES]
        qs_ref[p, :tq, :] = jnp.where(low_q, qp, zero_q)
        qs_ref[p, tq:, :] = jnp.where(low_q, zero_q, qp)
    acc_ref[...] = jnp.zeros_like(acc_ref)
    carry_ref[...] = jnp.zeros_like(carry_ref)

    row = lax.broadcasted_iota(jnp.int32, (2 * tq, tk), 0)
    col = lax.broadcasted_iota(jnp.int32, (2 * tq, tk), 1)
    q_off = jnp.where(row >= tq, row - tq, row)
    zero_k = jnp.zeros((tk, LANES), BF16)

    def tile(j, masked):
        ks = pl.multiple_of(j * tk, tk)
        if masked:
            strict = (ks + col) < (qi * tq + q_off)
        for p in range(PAIRS):
            kp = k_ref[0, pl.ds(ks, tk), p * LANES:(p + 1) * LANES]
            vp = v_ref[0, pl.ds(ks, tk), p * LANES:(p + 1) * LANES]
            z = _nt_dot(qs_ref[p], kp)
            sp = jnp.maximum(z, 0.0) + jnp.log(1.0 + jnp.exp(-jnp.abs(z)))
            if masked:
                sp = jnp.where(strict, sp, 0.0)
            ct = jnp.dot(sp.astype(BF16), tri, preferred_element_type=F32)
            e = (z - sp) - ct[:, :tk] - carry_ref[p]
            w = jnp.exp(e)
            if masked:
                w = jnp.where(strict, w, 0.0)
            carry_ref[p] = carry_ref[p] + ct[:, tk:]
            wb = w.astype(BF16)
            wcat = jnp.concatenate([wb[:tq], wb[tq:]], axis=1)
            vcat = jnp.concatenate([jnp.where(low_k, vp, zero_k),
                                    jnp.where(low_k, zero_k, vp)], axis=0)
            acc_ref[p] = acc_ref[p] + jnp.dot(wcat, vcat, preferred_element_type=F32)

    for d in reversed(range(nd)):
        tile(qi * nd + d, True)

    def body(it, carry):
        tile(qi * nd - 1 - it, False)
        return carry

    lax.fori_loop(0, qi * nd, body, 0)

    o = jnp.concatenate([acc_ref[p] for p in range(PAIRS)], axis=1)
    o_ref[0] = _rms(o, g_ref[...]).astype(BF16)


def _sb_attention(q, k, v, g):
    b, s, _ = q.shape
    return pl.pallas_call(
        _sb_kernel,
        grid=(b, s // SB_TQ),
        in_specs=[
            pl.BlockSpec((1, SB_TQ, SB_WIDTH), lambda bi, qi: (bi, qi, 0)),
            pl.BlockSpec((1, s, SB_WIDTH), lambda bi, qi: (bi, 0, 0)),
            pl.BlockSpec((1, s, SB_WIDTH), lambda bi, qi: (bi, 0, 0)),
            pl.BlockSpec((1, SB_WIDTH), lambda bi, qi: (0, 0)),
        ],
        out_specs=pl.BlockSpec((1, SB_TQ, SB_WIDTH), lambda bi, qi: (bi, qi, 0)),
        out_shape=jax.ShapeDtypeStruct((b, s, SB_WIDTH), BF16),
        scratch_shapes=[
            pltpu.VMEM((PAIRS, 2 * SB_TQ, LANES), BF16),
            pltpu.VMEM((PAIRS, SB_TQ, LANES), F32),
            pltpu.VMEM((PAIRS, 2 * SB_TQ, SB_TK), F32),
        ],
        compiler_params=pltpu.CompilerParams(
            dimension_semantics=("arbitrary", "arbitrary"), vmem_limit_bytes=VMEM_LIMIT),
        name="sb_attn",
    )(q, k, v, g)


def _sw_kernel(tab_ref, sink_ref, bkt_ref, q_ref, kp_ref, kc_ref, vp_ref, vc_ref, g_ref,
               o_ref, bias_ref):
    bi = pl.program_id(0)
    i = pl.program_id(1)
    qb = Q_BLOCK

    @pl.when((bi == 0) & (i == 0))
    def _():
        bk = bkt_ref[...]
        qrow = lax.broadcasted_iota(jnp.int32, (qb, KV_SPAN), 0)
        kcol = lax.broadcasted_iota(jnp.int32, (qb, KV_SPAN), 1)
        dchunk = qrow // CHUNK - (kcol // CHUNK - Q_BLOCK // CHUNK)
        band = (dchunk >= 0) & (dchunk <= WINDOW_CHUNKS)
        for h in range(SW_HEADS):
            acc = jnp.zeros((qb, KV_SPAN), F32)
            for bb in range(N_BUCKETS):
                acc = jnp.where(bk == bb, tab_ref[bb, h], acc)
            bias_ref[h // 2, (h % 2) * qb:(h % 2 + 1) * qb, :] = jnp.where(band, acc, NEG_INF)

    low_q = lax.broadcasted_iota(jnp.int32, (qb, LANES), 1) < HEAD_DIM
    low_k = lax.broadcasted_iota(jnp.int32, (KV_SPAN, LANES), 1) < HEAD_DIM
    kcol2 = lax.broadcasted_iota(jnp.int32, (2 * qb, KV_SPAN), 1)
    row2 = lax.broadcasted_iota(jnp.int32, (2 * qb, 1), 0)
    valid = (kcol2 >= qb) | (i > 0)
    zero_q = jnp.zeros((qb, LANES), BF16)
    zero_k = jnp.zeros((KV_SPAN, LANES), BF16)

    outs = []
    for p in range(SW_HEADS // 2):
        j = (2 * p) // (SW_HEADS // SW_KV_HEADS)
        qp = q_ref[0, :, p * LANES:(p + 1) * LANES]
        qstack = jnp.concatenate([jnp.where(low_q, qp, zero_q), jnp.where(low_q, zero_q, qp)], axis=0)
        kd = jnp.concatenate([kp_ref[0, :, j * LANES:(j + 1) * LANES],
                              kc_ref[0, :, j * LANES:(j + 1) * LANES]], axis=0)
        vd = jnp.concatenate([vp_ref[0, :, j * LANES:(j + 1) * LANES],
                              vc_ref[0, :, j * LANES:(j + 1) * LANES]], axis=0)
        s = _nt_dot(qstack, kd) + bias_ref[p]
        s = jnp.where(valid, s, NEG_INF)
        sink = jnp.where(row2 < qb, sink_ref[2 * p], sink_ref[2 * p + 1])
        m = jnp.maximum(jnp.max(s, axis=-1, keepdims=True), sink)
        pr = jnp.exp(s - m)
        den = jnp.sum(pr, axis=-1, keepdims=True) + jnp.exp(sink - m)
        pb = pr.astype(BF16)
        pcat = jnp.concatenate([pb[:qb], pb[qb:]], axis=1)
        vcat = jnp.concatenate([jnp.where(low_k, vd, zero_k), jnp.where(low_k, zero_k, vd)], axis=0)
        o = jnp.dot(pcat, vcat, preferred_element_type=F32)
        rinv = 1.0 / den
        outs.append(o * jnp.where(low_q, rinv[:qb], rinv[qb:]))
    o = jnp.concatenate(outs, axis=1)
    o_ref[0] = _rms(o, g_ref[...]).astype(BF16)


def _sw_attention(table, sinks, buckets, q, k, v, g):
    b, s, _ = q.shape
    nb = s // Q_BLOCK
    kvw = 2 * SW_KV_WIDTH
    prev = lambda bi, i: (bi, jnp.maximum(i - 1, 0), 0)
    cur = lambda bi, i: (bi, i, 0)
    return pl.pallas_call(
        _sw_kernel,
        grid=(b, nb),
        in_specs=[
            pl.BlockSpec(memory_space=pltpu.SMEM),
            pl.BlockSpec(memory_space=pltpu.SMEM),
            pl.BlockSpec((Q_BLOCK, KV_SPAN), lambda bi, i: (0, 0)),
            pl.BlockSpec((1, Q_BLOCK, SW_WIDTH), cur),
            pl.BlockSpec((1, Q_BLOCK, kvw), prev),
            pl.BlockSpec((1, Q_BLOCK, kvw), cur),
            pl.BlockSpec((1, Q_BLOCK, kvw), prev),
            pl.BlockSpec((1, Q_BLOCK, kvw), cur),
            pl.BlockSpec((1, SW_WIDTH), lambda bi, i: (0, 0)),
        ],
        out_specs=pl.BlockSpec((1, Q_BLOCK, SW_WIDTH), cur),
        out_shape=jax.ShapeDtypeStruct((b, s, SW_WIDTH), BF16),
        scratch_shapes=[pltpu.VMEM((SW_HEADS // 2, 2 * Q_BLOCK, KV_SPAN), F32)],
        compiler_params=pltpu.CompilerParams(
            dimension_semantics=("arbitrary", "arbitrary"), vmem_limit_bytes=VMEM_LIMIT),
        name="sw_attn",
    )(table, sinks, buckets, q, k, k, v, v, g)


def _post_kernel(sb_ref, sw_ref, x_ref, wo1_ref, wo2_ref, bo_ref, g_ref, wr_ref, br_ref,
                 x1_ref, h2p_ref, idx_ref, gate_ref, rank_ref, cnt_ref, carry_ref):
    i = pl.program_id(0)
    tm = POST_TM

    @pl.when(i == 0)
    def _():
        carry_ref[...] = jnp.zeros_like(carry_ref)

    y = (jnp.dot(sb_ref[...], wo1_ref[...], preferred_element_type=F32)
         + jnp.dot(sw_ref[...], wo2_ref[...], preferred_element_type=F32))
    x1 = x_ref[...] + (y + bo_ref[...])
    x1_ref[...] = x1
    h2 = _rms(x1, g_ref[...])
    h2p_ref[...] = _pack_halves(h2[:, :HALF], h2[:, HALF:])

    hh = h2.astype(BF16)
    hl = (h2 - hh.astype(F32)).astype(BF16)
    wr = wr_ref[...]
    wh = wr.astype(BF16)
    wl = (wr - wh.astype(F32)).astype(BF16)
    logits = _nt_dot(wh, hh) + (_nt_dot(wh, hl) + _nt_dot(wl, hh)) + br_ref[...]

    eid = lax.broadcasted_iota(jnp.int32, (N_EXPERTS, tm), 0)
    vals = logits
    sels, tops = [], []
    for k in range(TOP_K):
        m = jnp.max(vals, axis=0, keepdims=True)
        idx = jnp.min(jnp.where(vals == m, eid, N_EXPERTS), axis=0, keepdims=True)
        sel = eid == idx
        vals = jnp.where(sel, NEG_INF, vals)
        sels.append(sel)
        tops.append(m)
        idx_ref[k:k + 1, :] = idx
    ex = [jnp.exp(t - tops[0]) for t in tops]
    den = ex[0] + ex[1] + ex[2] + ex[3]
    for k in range(TOP_K):
        gate_ref[k:k + 1, :] = ex[k] / den

    multi = jnp.zeros((N_EXPERTS, tm), F32)
    for sel in sels:
        multi = multi + jnp.where(sel, 1.0, 0.0)
    r = lax.broadcasted_iota(jnp.int32, (tm, tm), 0)
    c = lax.broadcasted_iota(jnp.int32, (tm, tm), 1)
    upper = jnp.where(r < c, 1.0, 0.0).astype(BF16)
    prefix = jnp.dot(multi.astype(BF16), upper, preferred_element_type=F32)
    base = prefix + carry_ref[:, 0:1]
    for k in range(TOP_K):
        rk = jnp.sum(jnp.where(sels[k], base, 0.0), axis=0, keepdims=True)
        rank_ref[k:k + 1, :] = rk.astype(jnp.int32)
    carry_ref[...] = carry_ref[...] + jnp.sum(multi, axis=1, keepdims=True)
    cnt_ref[...] = carry_ref[...]


def _post(sb, sw, x2, wo1, wo2, bo, g, wr_t, br):
    t = x2.shape[0]
    tm = POST_TM
    row = lambda n: pl.BlockSpec((tm, n), lambda i: (i, 0))
    full = lambda a, b_: pl.BlockSpec((a, b_), lambda i: (0, 0))
    lane = lambda: pl.BlockSpec((TOP_K, tm), lambda i: (0, i))
    return pl.pallas_call(
        _post_kernel,
        grid=(t // tm,),
        in_specs=[row(SB_WIDTH), row(SW_WIDTH), row(D_MODEL),
                  full(SB_WIDTH, D_MODEL), full(SW_WIDTH, D_MODEL), full(1, D_MODEL), full(1, D_MODEL),
                  full(N_EXPERTS, D_MODEL), full(N_EXPERTS, 1)],
        out_specs=[row(D_MODEL), row(HALF), lane(), lane(), lane(), full(N_EXPERTS, LANES)],
        out_shape=[
            jax.ShapeDtypeStruct((t, D_MODEL), F32),
            jax.ShapeDtypeStruct((t, HALF), jnp.uint32),
            jax.ShapeDtypeStruct((TOP_K, t), jnp.int32),
            jax.ShapeDtypeStruct((TOP_K, t), F32),
            jax.ShapeDtypeStruct((TOP_K, t), jnp.int32),
            jax.ShapeDtypeStruct((N_EXPERTS, LANES), F32),
        ],
        scratch_shapes=[pltpu.VMEM((N_EXPERTS, LANES), F32)],
        compiler_params=pltpu.CompilerParams(
            dimension_semantics=("arbitrary",), vmem_limit_bytes=VMEM_LIMIT),
        name="post",
    )(sb, sw, x2, wo1, wo2, bo, g, wr_t, br)


def _dest_kernel(pstart_ref, idx_ref, rank_ref, dest_ref):
    idx = idx_ref[...]
    off = jnp.zeros(idx.shape, jnp.int32)
    for e in range(N_EXPERTS):
        off = jnp.where(idx == e, pstart_ref[e], off)
    dest_ref[...] = off + rank_ref[...]


def _dest(pstart, idx, rank):
    k, t = idx.shape
    tm = 2048
    blk = lambda: pl.BlockSpec((k, tm), lambda i: (0, i))
    return pl.pallas_call(
        _dest_kernel,
        grid=(t // tm,),
        in_specs=[pl.BlockSpec(memory_space=pltpu.SMEM), blk(), blk()],
        out_specs=blk(),
        out_shape=jax.ShapeDtypeStruct((k, t), jnp.int32),
        name="dest",
    )(pstart, idx, rank)


def _dispatch_kernel(dest_ref, h_ref, xs_in_ref, xs_ref, sem):
    del xs_in_ref
    tm = DISP_TM

    def issue(t, carry):
        for k in range(TOP_K):
            d = dest_ref[k, t]
            pltpu.make_async_copy(h_ref.at[pl.ds(t, 1)], xs_ref.at[pl.ds(d, 1)], sem).start()
        return carry

    lax.fori_loop(0, tm, issue, 0)
    for k in range(TOP_K):
        pltpu.make_async_copy(h_ref, xs_ref.at[pl.ds(0, tm)], sem).wait()


def _dispatch(dest, h2p, xs_zero):
    t = h2p.shape[0]
    tm = DISP_TM
    return pl.pallas_call(
        _dispatch_kernel,
        grid=(t // tm,),
        in_specs=[
            pl.BlockSpec((TOP_K, tm), lambda i: (0, i), memory_space=pltpu.SMEM),
            pl.BlockSpec((tm, HALF), lambda i: (i, 0)),
            pl.BlockSpec(memory_space=pl.ANY),
        ],
        out_specs=pl.BlockSpec(memory_space=pl.ANY),
        out_shape=jax.ShapeDtypeStruct(xs_zero.shape, jnp.uint32),
        scratch_shapes=[pltpu.SemaphoreType.DMA],
        input_output_aliases={2: 0},
        compiler_params=pltpu.CompilerParams(
            dimension_semantics=("arbitrary",), has_side_effects=True, disable_bounds_checks=True),
        name="dispatch",
    )(dest, h2p, xs_zero)


def _expert_kernel(be_ref, nb_ref, xs_ref, wgu_ref, bgu_ref, wd_ref, bd_ref, ys_ref, wgu_bf, wd_bf):
    i = pl.program_id(0)
    e = be_ref[i]
    prev = be_ref[jnp.maximum(i - 1, 0)]

    @pl.when((i == 0) | (e != prev))
    def _():
        rows = 128
        def cast(c, carry):
            r0 = pl.multiple_of(c * rows, rows)
            wgu_bf[pl.ds(r0, rows), :] = wgu_ref[0, pl.ds(r0, rows), :].astype(BF16)
            wd_bf[pl.ds(r0, rows), :] = wd_ref[0, pl.ds(r0, rows), :].astype(BF16)
            return carry
        lax.fori_loop(0, D_MODEL // rows, cast, 0)

    @pl.when(i < nb_ref[0])
    def _():
        lo, hi = _unpack_halves(xs_ref[...])
        xlo = lo.astype(BF16)
        xhi = hi.astype(BF16)

        def proj(c0):
            return (jnp.dot(xlo, wgu_bf[:HALF, c0:c0 + FF_CHUNK], preferred_element_type=F32)
                    + jnp.dot(xhi, wgu_bf[HALF:, c0:c0 + FF_CHUNK], preferred_element_type=F32)
                    + bgu_ref[0, :, c0:c0 + FF_CHUNK])

        y = jnp.zeros((EXP_BLK, D_MODEL), F32)
        for c in range(D_FF // FF_CHUNK):
            gate = jnp.minimum(proj(c * FF_CHUNK), SWIGLU_LIMIT)
            up = jnp.clip(proj(D_FF + c * FF_CHUNK), -SWIGLU_LIMIT, SWIGLU_LIMIT)
            act = (up + 1.0) * (gate * (1.0 / (1.0 + jnp.exp(-SWIGLU_ALPHA * gate))))
            y = y + jnp.dot(act.astype(BF16), wd_bf[c * FF_CHUNK:(c + 1) * FF_CHUNK, :],
                            preferred_element_type=F32)
        y = y + bd_ref[0]
        ys_ref[...] = _pack_halves(y[:, :HALF], y[:, HALF:])

    @pl.when(i >= nb_ref[0])
    def _():
        ys_ref[...] = jnp.zeros_like(ys_ref)


def _experts(block_e, n_blocks, xs, wgu, bgu, wd, bd):
    p_rows = xs.shape[0]
    nb = p_rows // EXP_BLK
    xblk = lambda i, be, nbr: (jnp.minimum(i, nbr[0] - 1), 0)
    wblk = lambda i, be, nbr: (be[i], 0, 0)
    grid_spec = pltpu.PrefetchScalarGridSpec(
        num_scalar_prefetch=2,
        grid=(nb,),
        in_specs=[
            pl.BlockSpec((EXP_BLK, HALF), xblk),
            pl.BlockSpec((1, D_MODEL, 2 * D_FF), wblk),
            pl.BlockSpec((1, 1, 2 * D_FF), wblk),
            pl.BlockSpec((1, D_FF, D_MODEL), wblk),
            pl.BlockSpec((1, 1, D_MODEL), wblk),
        ],
        out_specs=pl.BlockSpec((EXP_BLK, HALF), lambda i, be, nbr: (i, 0)),
        scratch_shapes=[pltpu.VMEM((D_MODEL, 2 * D_FF), BF16), pltpu.VMEM((D_FF, D_MODEL), BF16)],
    )
    return pl.pallas_call(
        _expert_kernel,
        grid_spec=grid_spec,
        out_shape=jax.ShapeDtypeStruct((p_rows, HALF), jnp.uint32),
        compiler_params=pltpu.CompilerParams(
            dimension_semantics=("arbitrary",), vmem_limit_bytes=VMEM_LIMIT),
        name="experts",
    )(block_e, n_blocks, xs, wgu, bgu, wd, bd)


def _combine_kernel(dest_ref, ys_ref, x1_ref, gate_ref, g_ref, o_ref, buf, sem, *, final):
    tm = COMB_TM

    def issue(t, carry):
        for k in range(TOP_K):
            d = dest_ref[k, t]
            pltpu.make_async_copy(ys_ref.at[pl.ds(d, 1)], buf.at[k, pl.ds(t, 1)], sem).start()
        return carry

    lax.fori_loop(0, tm, issue, 0)
    for k in range(TOP_K):
        pltpu.make_async_copy(ys_ref.at[pl.ds(0, tm)], buf.at[k], sem).wait()

    gates = gate_ref[...]
    acc_lo = jnp.zeros((tm, HALF), F32)
    acc_hi = jnp.zeros((tm, HALF), F32)
    for k in range(TOP_K):
        lo, hi = _unpack_halves(buf[k])
        gk = gates[:, k:k + 1]
        acc_lo = acc_lo + gk * lo
        acc_hi = acc_hi + gk * hi
    x2 = x1_ref[...] + jnp.concatenate([acc_lo, acc_hi], axis=1)
    o_ref[...] = _rms(x2, g_ref[...]) if final else x2


def _combine(dest, ys, x1, gates_t, g, final):
    t = x1.shape[0]
    tm = COMB_TM
    return pl.pallas_call(
        functools.partial(_combine_kernel, final=final),
        grid=(t // tm,),
        in_specs=[
            pl.BlockSpec((TOP_K, tm), lambda i: (0, i), memory_space=pltpu.SMEM),
            pl.BlockSpec(memory_space=pl.ANY),
            pl.BlockSpec((tm, D_MODEL), lambda i: (i, 0)),
            pl.BlockSpec((tm, TOP_K), lambda i: (i, 0)),
            pl.BlockSpec((1, D_MODEL), lambda i: (0, 0)),
        ],
        out_specs=pl.BlockSpec((tm, D_MODEL), lambda i: (i, 0)),
        out_shape=jax.ShapeDtypeStruct((t, D_MODEL), F32),
        scratch_shapes=[pltpu.VMEM((TOP_K, tm, HALF), jnp.uint32), pltpu.SemaphoreType.DMA],
        compiler_params=pltpu.CompilerParams(
            dimension_semantics=("arbitrary",), disable_bounds_checks=True),
        name="combine",
    )(dest, ys, x1, gates_t, g)


def _t5_buckets():
    qi = jnp.arange(Q_BLOCK)
    kj = jnp.arange(KV_SPAN) - Q_BLOCK
    rel = kj[None, :] - qi[:, None]
    nb = N_BUCKETS // 2
    max_exact = nb // 2
    bucket = jnp.where(rel > 0, nb, 0)
    n = jnp.abs(rel)
    nf = jnp.maximum(n, 1).astype(jnp.float32)
    large = max_exact + (jnp.log(nf / max_exact) / math.log(MAX_DISTANCE / max_exact)
                         * (nb - max_exact)).astype(jnp.int32)
    large = jnp.minimum(large, nb - 1)
    return (bucket + jnp.where(n < max_exact, n, large)).astype(jnp.int32)


def _dup_heads(w):
    parts = []
    for j in range(SW_KV_HEADS):
        hj = w[..., j * HEAD_DIM:(j + 1) * HEAD_DIM]
        parts += [hj, hj]
    return jnp.concatenate(parts, axis=-1)


def kernel(x, attn_norm, w_in, b_in, sb_out_norm, sw_out_norm, sinks, rel_bias_table, w_out, b_out,
           ffn_norm, w_router, b_router, w_gate_up, b_gate_up, w_down, b_down, final_norm):
    bsz, seq, _ = x.shape
    t = bsz * seq
    depth = w_in.shape[0]
    scale = HEAD_DIM ** -0.5
    buckets = _t5_buckets()
    table = rel_bias_table.astype(F32)
    n_rows = t * TOP_K + N_EXPERTS * EXP_BLK
    n_blk = n_rows // EXP_BLK

    x2 = x.reshape(t, D_MODEL)
    for l in range(depth):
        def cols(a):
            o1, o2, o3 = SB_WIDTH, 2 * SB_WIDTH, 3 * SB_WIDTH
            o4 = o3 + SW_WIDTH
            o5 = o4 + SW_KV_WIDTH
            return jnp.concatenate([
                a[..., :o1] * scale, a[..., o1:o2], a[..., o2:o3], a[..., o3:o4] * scale,
                _dup_heads(a[..., o4:o5]), _dup_heads(a[..., o5:])], axis=-1)

        w_l = cols(w_in[l]).astype(BF16)
        b_l = cols(b_in[l]).reshape(1, -1).astype(F32)
        widths = (SB_WIDTH, SB_WIDTH, SB_WIDTH, SW_WIDTH, 2 * SW_KV_WIDTH, 2 * SW_KV_WIDTH)
        sbq, sbk, sbv, swq, swk, swv = _inproj(x2, attn_norm[l].reshape(1, -1), w_l, b_l, widths)

        r3 = lambda a: a.reshape(bsz, seq, a.shape[-1])
        sb = _sb_attention(r3(sbq), r3(sbk), r3(sbv), sb_out_norm[l].reshape(1, -1))
        sw = _sw_attention(table, sinks[l].astype(F32), buckets, r3(swq), r3(swk), r3(swv),
                           sw_out_norm[l].reshape(1, -1))

        wo = w_out[l].astype(BF16)
        x1, h2p, top_idx, gates, rank, counts = _post(
            sb.reshape(t, SB_WIDTH), sw.reshape(t, SW_WIDTH), x2,
            wo[:SB_WIDTH], wo[SB_WIDTH:], b_out[l].reshape(1, -1), ffn_norm[l].reshape(1, -1),
            w_router[l].T, b_router[l].reshape(-1, 1).astype(F32))

        cnt = counts[:, 0].astype(jnp.int32)
        blocks_e = (cnt + EXP_BLK - 1) // EXP_BLK
        blk_end = jnp.cumsum(blocks_e)
        pstart = (blk_end - blocks_e) * EXP_BLK
        n_used = blk_end[-1:]
        last_e = jnp.max(jnp.where(blocks_e > 0, jnp.arange(N_EXPERTS), 0))
        block_e = jnp.searchsorted(blk_end, jnp.arange(n_blk), side="right")
        block_e = jnp.minimum(block_e, last_e).astype(jnp.int32)

        dest = _dest(pstart.astype(jnp.int32), top_idx, rank)
        xs = _dispatch(dest, h2p, jnp.zeros((n_rows, HALF), jnp.uint32))
        ys = _experts(block_e, n_used.astype(jnp.int32), xs, w_gate_up[l],
                      b_gate_up[l].reshape(N_EXPERTS, 1, -1), w_down[l], b_down[l].reshape(N_EXPERTS, 1, -1))
        x2 = _combine(dest, ys, x1, gates.T, final_norm.reshape(1, -1), final=(l + 1 == depth))
    return x2.reshape(bsz, seq, D_MODEL)
```

```python
import functools
import math

import jax
import jax.numpy as jnp
from jax import lax
from jax.experimental import pallas as pl
from jax.experimental.pallas import tpu as pltpu

D_MODEL = 1024
HEAD_DIM = 64
SB_HEADS = 8
SW_HEADS = 8
SW_KV_HEADS = 2
SB_WIDTH = SB_HEADS * HEAD_DIM
SW_WIDTH = SW_HEADS * HEAD_DIM
SW_KV_WIDTH = SW_KV_HEADS * HEAD_DIM
Q_BLOCK = 128
KV_SPAN = 2 * Q_BLOCK
CHUNK = 64
WINDOW_CHUNKS = 2
N_BUCKETS = 32
MAX_DISTANCE = 128
N_EXPERTS = 32
TOP_K = 4
D_FF = 1024
SWIGLU_LIMIT = 7.0
SWIGLU_ALPHA = 1.702
RMS_EPS = 1e-5

LANES = 128
HALF = D_MODEL // 2
PAIRS = SB_HEADS // 2

IN_TM = 512
SB_T = 256
LOG2E = math.log2(math.e)
POST_TM = 512
EXP_BLK = 256
FF_CHUNK = 512
DISP_TM = 256
COMB_TM = 128
ISSUE_UNROLL = 8
VMEM_LIMIT = 56 * 1024 * 1024

F32 = jnp.float32
BF16 = jnp.bfloat16
NEG_INF = float("-inf")


def _nt_dot(a, b):
    return lax.dot_general(a, b, (((1,), (1,)), ((), ())), preferred_element_type=F32)


def _rms(x, g):
    ms = jnp.mean(x * x, axis=-1, keepdims=True)
    return x * lax.rsqrt(ms + RMS_EPS) * g


def _neg_abs(x):
    bits = lax.bitcast_convert_type(x, jnp.uint32) | jnp.uint32(0x80000000)
    return lax.bitcast_convert_type(bits, F32)


def _pack_halves(lo, hi):
    lo_b = lax.bitcast_convert_type(lo.astype(BF16).astype(F32), jnp.uint32)
    hi_b = lax.bitcast_convert_type(hi.astype(BF16).astype(F32), jnp.uint32)
    return (lo_b >> 16) | hi_b


def _unpack_halves(p):
    lo = lax.bitcast_convert_type(p << 16, F32)
    hi = lax.bitcast_convert_type(p & jnp.uint32(0xFFFF0000), F32)
    return lo, hi


def _inproj_kernel(x_ref, g_ref, w_ref, b_ref, *out_refs):
    hb = _rms(x_ref[...], g_ref[...]).astype(BF16)
    c0 = 0
    for ref in out_refs:
        n = ref.shape[-1]
        acc = jnp.dot(hb, w_ref[:, c0:c0 + n], preferred_element_type=F32)
        ref[...] = (acc + b_ref[:, c0:c0 + n]).astype(BF16)
        c0 += n


def _inproj(x2, g, w, b, widths):
    t = x2.shape[0]
    n_all = sum(widths)
    return pl.pallas_call(
        _inproj_kernel,
        grid=(t // IN_TM,),
        in_specs=[
            pl.BlockSpec((IN_TM, D_MODEL), lambda i: (i, 0)),
            pl.BlockSpec((1, D_MODEL), lambda i: (0, 0)),
            pl.BlockSpec((D_MODEL, n_all), lambda i: (0, 0)),
            pl.BlockSpec((1, n_all), lambda i: (0, 0)),
        ],
        out_specs=[pl.BlockSpec((IN_TM, n), lambda i: (i, 0)) for n in widths],
        out_shape=[jax.ShapeDtypeStruct((t, n), BF16) for n in widths],
        compiler_params=pltpu.CompilerParams(
            dimension_semantics=("arbitrary",), vmem_limit_bytes=VMEM_LIMIT),
        name="inproj",
    )(x2, g, w, b)


def _sb_kernel(q_ref, k_ref, v_ref, g_ref, o_ref, qs_ref, acc_ref, carry_ref):
    qi = pl.program_id(1)
    t = SB_T

    low_q = lax.broadcasted_iota(jnp.int32, (t, LANES), 1) < HEAD_DIM
    low_k = lax.broadcasted_iota(jnp.int32, (t, LANES), 1) < HEAD_DIM
    r = lax.broadcasted_iota(jnp.int32, (t, t), 0)
    c = lax.broadcasted_iota(jnp.int32, (t, t), 1)
    tri = jnp.where(r > c, 1.0, 0.0).astype(BF16)

    zero = jnp.zeros((t, LANES), BF16)
    for p in range(PAIRS):
        qp = q_ref[0, :, p * LANES:(p + 1) * LANES]
        qs_ref[p, :t, :] = jnp.where(low_q, qp, zero)
        qs_ref[p, t:, :] = jnp.where(low_q, zero, qp)
    acc_ref[...] = jnp.zeros_like(acc_ref)
    carry_ref[...] = jnp.zeros_like(carry_ref)

    def tile(j, masked):
        ks = pl.multiple_of(j * t, t)
        if masked:
            row = lax.broadcasted_iota(jnp.int32, (2 * t, t), 0)
            col = lax.broadcasted_iota(jnp.int32, (2 * t, t), 1)
            strict = col < jnp.where(row >= t, row - t, row)
        for p in range(PAIRS):
            kp = k_ref[0, pl.ds(ks, t), p * LANES:(p + 1) * LANES]
            vp = v_ref[0, pl.ds(ks, t), p * LANES:(p + 1) * LANES]
            z = _nt_dot(qs_ref[p], kp)
            sp = jnp.maximum(z, 0.0) + jnp.log(1.0 + jnp.exp2(_neg_abs(z))) * LOG2E
            if masked:
                sp = jnp.where(strict, sp, 0.0)
            cx = jnp.dot(sp.astype(BF16), tri, preferred_element_type=F32)
            cr = carry_ref[p]
            e = (z - sp) - cx - jnp.concatenate([cr, cr], axis=1)
            w = jnp.exp2(e)
            if masked:
                w = jnp.where(strict, w, 0.0)
            carry_ref[p] = cr + jnp.sum(sp, axis=-1, keepdims=True)
            wb = w.astype(BF16)
            wcat = jnp.concatenate([wb[:t], wb[t:]], axis=1)
            vcat = jnp.concatenate([jnp.where(low_k, vp, zero),
                                    jnp.where(low_k, zero, vp)], axis=0)
            acc_ref[p] = acc_ref[p] + jnp.dot(wcat, vcat, preferred_element_type=F32)

    tile(qi, True)

    def body(it, carry):
        tile(qi - 1 - it, False)
        return carry

    lax.fori_loop(0, qi, body, 0)

    o = jnp.concatenate([acc_ref[p] for p in range(PAIRS)], axis=1)
    o_ref[0] = _rms(o, g_ref[...]).astype(BF16)


def _sb_attention(q, k, v, g):
    b, s, _ = q.shape
    return pl.pallas_call(
        _sb_kernel,
        grid=(b, s // SB_T),
        in_specs=[
            pl.BlockSpec((1, SB_T, SB_WIDTH), lambda bi, qi: (bi, qi, 0)),
            pl.BlockSpec((1, s, SB_WIDTH), lambda bi, qi: (bi, 0, 0)),
            pl.BlockSpec((1, s, SB_WIDTH), lambda bi, qi: (bi, 0, 0)),
            pl.BlockSpec((1, SB_WIDTH), lambda bi, qi: (0, 0)),
        ],
        out_specs=pl.BlockSpec((1, SB_T, SB_WIDTH), lambda bi, qi: (bi, qi, 0)),
        out_shape=jax.ShapeDtypeStruct((b, s, SB_WIDTH), BF16),
        scratch_shapes=[
            pltpu.VMEM((PAIRS, 2 * SB_T, LANES), BF16),
            pltpu.VMEM((PAIRS, SB_T, LANES), F32),
            pltpu.VMEM((PAIRS, 2 * SB_T, LANES), F32),
        ],
        compiler_params=pltpu.CompilerParams(
            dimension_semantics=("arbitrary", "arbitrary"), vmem_limit_bytes=VMEM_LIMIT),
        name="sb_attn",
    )(q, k, v, g)


def _sw_kernel(tab_ref, sink_ref, bkt_ref, q_ref, kp_ref, kc_ref, vp_ref, vc_ref, g_ref,
               o_ref, bias_ref):
    bi = pl.program_id(0)
    i = pl.program_id(1)
    qb = Q_BLOCK

    @pl.when((bi == 0) & (i == 0))
    def _():
        bk = bkt_ref[...]
        qrow = lax.broadcasted_iota(jnp.int32, (qb, KV_SPAN), 0)
        kcol = lax.broadcasted_iota(jnp.int32, (qb, KV_SPAN), 1)
        dchunk = qrow // CHUNK - (kcol // CHUNK - Q_BLOCK // CHUNK)
        band = (dchunk >= 0) & (dchunk <= WINDOW_CHUNKS)
        for h in range(SW_HEADS):
            acc = jnp.zeros((qb, KV_SPAN), F32)
            for bb in range(N_BUCKETS):
                acc = jnp.where(bk == bb, tab_ref[bb, h], acc)
            bias_ref[h // 2, (h % 2) * qb:(h % 2 + 1) * qb, :] = jnp.where(band, acc, NEG_INF)

    low_q = lax.broadcasted_iota(jnp.int32, (qb, LANES), 1) < HEAD_DIM
    low_k = lax.broadcasted_iota(jnp.int32, (KV_SPAN, LANES), 1) < HEAD_DIM
    kcol2 = lax.broadcasted_iota(jnp.int32, (2 * qb, KV_SPAN), 1)
    row2 = lax.broadcasted_iota(jnp.int32, (2 * qb, 1), 0)
    valid = (kcol2 >= qb) | (i > 0)
    zero_q = jnp.zeros((qb, LANES), BF16)
    zero_k = jnp.zeros((KV_SPAN, LANES), BF16)

    outs = []
    for p in range(SW_HEADS // 2):
        j = (2 * p) // (SW_HEADS // SW_KV_HEADS)
        qp = q_ref[0, :, p * LANES:(p + 1) * LANES]
        qstack = jnp.concatenate([jnp.where(low_q, qp, zero_q), jnp.where(low_q, zero_q, qp)], axis=0)
        kd = jnp.concatenate([kp_ref[0, :, j * LANES:(j + 1) * LANES],
                              kc_ref[0, :, j * LANES:(j + 1) * LANES]], axis=0)
        vd = jnp.concatenate([vp_ref[0, :, j * LANES:(j + 1) * LANES],
                              vc_ref[0, :, j * LANES:(j + 1) * LANES]], axis=0)
        s = _nt_dot(qstack, kd) + bias_ref[p]
        s = jnp.where(valid, s, NEG_INF)
        sink = jnp.where(row2 < qb, sink_ref[2 * p], sink_ref[2 * p + 1])
        m = jnp.maximum(jnp.max(s, axis=-1, keepdims=True), sink)
        pr = jnp.exp(s - m)
        den = jnp.sum(pr, axis=-1, keepdims=True) + jnp.exp(sink - m)
        pb = pr.astype(BF16)
        pcat = jnp.concatenate([pb[:qb], pb[qb:]], axis=1)
        vcat = jnp.concatenate([jnp.where(low_k, vd, zero_k), jnp.where(low_k, zero_k, vd)], axis=0)
        o = jnp.dot(pcat, vcat, preferred_element_type=F32)
        rinv = 1.0 / den
        outs.append(o * jnp.where(low_q, rinv[:qb], rinv[qb:]))
    o = jnp.concatenate(outs, axis=1)
    o_ref[0] = _rms(o, g_ref[...]).astype(BF16)


def _sw_attention(table, sinks, buckets, q, k, v, g):
    b, s, _ = q.shape
    nb = s // Q_BLOCK
    kvw = 2 * SW_KV_WIDTH
    prev = lambda bi, i: (bi, jnp.maximum(i - 1, 0), 0)
    cur = lambda bi, i: (bi, i, 0)
    return pl.pallas_call(
        _sw_kernel,
        grid=(b, nb),
        in_specs=[
            pl.BlockSpec(memory_space=pltpu.SMEM),
            pl.BlockSpec(memory_space=pltpu.SMEM),
            pl.BlockSpec((Q_BLOCK, KV_SPAN), lambda bi, i: (0, 0)),
            pl.BlockSpec((1, Q_BLOCK, SW_WIDTH), cur),
            pl.BlockSpec((1, Q_BLOCK, kvw), prev),
            pl.BlockSpec((1, Q_BLOCK, kvw), cur),
            pl.BlockSpec((1, Q_BLOCK, kvw), prev),
            pl.BlockSpec((1, Q_BLOCK, kvw), cur),
            pl.BlockSpec((1, SW_WIDTH), lambda bi, i: (0, 0)),
        ],
        out_specs=pl.BlockSpec((1, Q_BLOCK, SW_WIDTH), cur),
        out_shape=jax.ShapeDtypeStruct((b, s, SW_WIDTH), BF16),
        scratch_shapes=[pltpu.VMEM((SW_HEADS // 2, 2 * Q_BLOCK, KV_SPAN), F32)],
        compiler_params=pltpu.CompilerParams(
            dimension_semantics=("arbitrary", "arbitrary"), vmem_limit_bytes=VMEM_LIMIT),
        name="sw_attn",
    )(table, sinks, buckets, q, k, k, v, v, g)


def _post_kernel(sb_ref, sw_ref, x_ref, wo1_ref, wo2_ref, bo_ref, g_ref, wr_ref, br_ref,
                 x1_ref, h2p_ref, idx_ref, gate_ref, rank_ref, cnt_ref, carry_ref):
    i = pl.program_id(0)
    tm = POST_TM

    @pl.when(i == 0)
    def _():
        carry_ref[...] = jnp.zeros_like(carry_ref)

    y = (jnp.dot(sb_ref[...], wo1_ref[...], preferred_element_type=F32)
         + jnp.dot(sw_ref[...], wo2_ref[...], preferred_element_type=F32))
    x1 = x_ref[...] + (y + bo_ref[...])
    x1_ref[...] = x1
    h2 = _rms(x1, g_ref[...])
    h2p_ref[...] = _pack_halves(h2[:, :HALF], h2[:, HALF:])

    hh = h2.astype(BF16)
    hl = (h2 - hh.astype(F32)).astype(BF16)
    wr = wr_ref[...]
    wh = wr.astype(BF16)
    wl = (wr - wh.astype(F32)).astype(BF16)
    logits = _nt_dot(wh, hh) + (_nt_dot(wh, hl) + _nt_dot(wl, hh)) + br_ref[...]

    eid = lax.broadcasted_iota(jnp.int32, (N_EXPERTS, tm), 0)
    vals = logits
    sels, tops = [], []
    for k in range(TOP_K):
        m = jnp.max(vals, axis=0, keepdims=True)
        idx = jnp.min(jnp.where(vals == m, eid, N_EXPERTS), axis=0, keepdims=True)
        sel = eid == idx
        vals = jnp.where(sel, NEG_INF, vals)
        sels.append(sel)
        tops.append(m)
        idx_ref[k:k + 1, :] = idx
    ex = [jnp.exp(t - tops[0]) for t in tops]
    den = ex[0] + ex[1] + ex[2] + ex[3]
    for k in range(TOP_K):
        gate_ref[k:k + 1, :] = ex[k] / den

    multi = jnp.zeros((N_EXPERTS, tm), F32)
    for sel in sels:
        multi = multi + jnp.where(sel, 1.0, 0.0)
    r = lax.broadcasted_iota(jnp.int32, (tm, tm), 0)
    c = lax.broadcasted_iota(jnp.int32, (tm, tm), 1)
    upper = jnp.where(r < c, 1.0, 0.0).astype(BF16)
    prefix = jnp.dot(multi.astype(BF16), upper, preferred_element_type=F32)
    base = prefix + carry_ref[:, 0:1]
    for k in range(TOP_K):
        rk = jnp.sum(jnp.where(sels[k], base, 0.0), axis=0, keepdims=True)
        rank_ref[k:k + 1, :] = rk.astype(jnp.int32)
    carry_ref[...] = carry_ref[...] + jnp.sum(multi, axis=1, keepdims=True)
    cnt_ref[...] = carry_ref[...]


def _post(sb, sw, x2, wo1, wo2, bo, g, wr_t, br):
    t = x2.shape[0]
    tm = POST_TM
    row = lambda n: pl.BlockSpec((tm, n), lambda i: (i, 0))
    full = lambda a, b_: pl.BlockSpec((a, b_), lambda i: (0, 0))
    lane = lambda: pl.BlockSpec((TOP_K, tm), lambda i: (0, i))
    return pl.pallas_call(
        _post_kernel,
        grid=(t // tm,),
        in_specs=[row(SB_WIDTH), row(SW_WIDTH), row(D_MODEL),
                  full(SB_WIDTH, D_MODEL), full(SW_WIDTH, D_MODEL), full(1, D_MODEL), full(1, D_MODEL),
                  full(N_EXPERTS, D_MODEL), full(N_EXPERTS, 1)],
        out_specs=[row(D_MODEL), row(HALF), lane(), lane(), lane(), full(N_EXPERTS, LANES)],
        out_shape=[
            jax.ShapeDtypeStruct((t, D_MODEL), F32),
            jax.ShapeDtypeStruct((t, HALF), jnp.uint32),
            jax.ShapeDtypeStruct((TOP_K, t), jnp.int32),
            jax.ShapeDtypeStruct((TOP_K, t), F32),
            jax.ShapeDtypeStruct((TOP_K, t), jnp.int32),
            jax.ShapeDtypeStruct((N_EXPERTS, LANES), F32),
        ],
        scratch_shapes=[pltpu.VMEM((N_EXPERTS, LANES), F32)],
        compiler_params=pltpu.CompilerParams(
            dimension_semantics=("arbitrary",), vmem_limit_bytes=VMEM_LIMIT),
        name="post",
    )(sb, sw, x2, wo1, wo2, bo, g, wr_t, br)


def _dest_kernel(pstart_ref, idx_ref, rank_ref, dest_ref):
    idx = idx_ref[...]
    off = jnp.zeros(idx.shape, jnp.int32)
    for e in range(N_EXPERTS):
        off = jnp.where(idx == e, pstart_ref[e], off)
    dest_ref[...] = off + rank_ref[...]


def _dest(pstart, idx, rank):
    k, t = idx.shape
    tm = 2048
    blk = lambda: pl.BlockSpec((k, tm), lambda i: (0, i))
    return pl.pallas_call(
        _dest_kernel,
        grid=(t // tm,),
        in_specs=[pl.BlockSpec(memory_space=pltpu.SMEM), blk(), blk()],
        out_specs=blk(),
        out_shape=jax.ShapeDtypeStruct((k, t), jnp.int32),
        name="dest",
    )(pstart, idx, rank)


def _dispatch_kernel(dest_ref, h_ref, xs_in_ref, xs_ref, sem):
    del xs_in_ref
    tm = DISP_TM

    def issue(t, carry):
        for k in range(TOP_K):
            d = dest_ref[k, t]
            pltpu.make_async_copy(h_ref.at[pl.ds(t, 1)], xs_ref.at[pl.ds(d, 1)], sem).start()
        return carry

    lax.fori_loop(0, tm, issue, 0, unroll=ISSUE_UNROLL)
    for k in range(TOP_K):
        pltpu.make_async_copy(h_ref, xs_ref.at[pl.ds(0, tm)], sem).wait()


def _dispatch(dest, h2p, xs_zero):
    t = h2p.shape[0]
    tm = DISP_TM
    return pl.pallas_call(
        _dispatch_kernel,
        grid=(t // tm,),
        in_specs=[
            pl.BlockSpec((TOP_K, tm), lambda i: (0, i), memory_space=pltpu.SMEM),
            pl.BlockSpec((tm, HALF), lambda i: (i, 0)),
            pl.BlockSpec(memory_space=pl.ANY),
        ],
        out_specs=pl.BlockSpec(memory_space=pl.ANY),
        out_shape=jax.ShapeDtypeStruct(xs_zero.shape, jnp.uint32),
        scratch_shapes=[pltpu.SemaphoreType.DMA],
        input_output_aliases={2: 0},
        compiler_params=pltpu.CompilerParams(
            dimension_semantics=("arbitrary",), has_side_effects=True, disable_bounds_checks=True),
        name="dispatch",
    )(dest, h2p, xs_zero)


def _expert_kernel(be_ref, nb_ref, xs_ref, wgu_ref, bgu_ref, wd_ref, bd_ref, ys_ref, wgu_bf, wd_bf):
    i = pl.program_id(0)
    e = be_ref[i]
    prev = be_ref[jnp.maximum(i - 1, 0)]

    @pl.when((i == 0) | (e != prev))
    def _():
        rows = 128
        def cast(c, carry):
            r0 = pl.multiple_of(c * rows, rows)
            wgu_bf[pl.ds(r0, rows), :] = wgu_ref[0, pl.ds(r0, rows), :].astype(BF16)
            wd_bf[pl.ds(r0, rows), :] = wd_ref[0, pl.ds(r0, rows), :].astype(BF16)
            return carry
        lax.fori_loop(0, D_MODEL // rows, cast, 0)

    @pl.when(i < nb_ref[0])
    def _():
        lo, hi = _unpack_halves(xs_ref[...])
        xlo = lo.astype(BF16)
        xhi = hi.astype(BF16)

        def proj(c0):
            return (jnp.dot(xlo, wgu_bf[:HALF, c0:c0 + FF_CHUNK], preferred_element_type=F32)
                    + jnp.dot(xhi, wgu_bf[HALF:, c0:c0 + FF_CHUNK], preferred_element_type=F32)
                    + bgu_ref[0, :, c0:c0 + FF_CHUNK])

        y = jnp.zeros((EXP_BLK, D_MODEL), F32)
        for c in range(D_FF // FF_CHUNK):
            gate = jnp.minimum(proj(c * FF_CHUNK), SWIGLU_LIMIT)
            up = jnp.clip(proj(D_FF + c * FF_CHUNK), -SWIGLU_LIMIT, SWIGLU_LIMIT)
            act = (up + 1.0) * (gate * (1.0 / (1.0 + jnp.exp(-SWIGLU_ALPHA * gate))))
            y = y + jnp.dot(act.astype(BF16), wd_bf[c * FF_CHUNK:(c + 1) * FF_CHUNK, :],
                            preferred_element_type=F32)
        y = y + bd_ref[0]
        ys_ref[...] = _pack_halves(y[:, :HALF], y[:, HALF:])

    @pl.when(i >= nb_ref[0])
    def _():
        ys_ref[...] = jnp.zeros_like(ys_ref)


def _experts(block_e, n_blocks, xs, wgu, bgu, wd, bd):
    p_rows = xs.shape[0]
    nb = p_rows // EXP_BLK
    xblk = lambda i, be, nbr: (jnp.minimum(i, nbr[0] - 1), 0)
    wblk = lambda i, be, nbr: (be[i], 0, 0)
    grid_spec = pltpu.PrefetchScalarGridSpec(
        num_scalar_prefetch=2,
        grid=(nb,),
        in_specs=[
            pl.BlockSpec((EXP_BLK, HALF), xblk),
            pl.BlockSpec((1, D_MODEL, 2 * D_FF), wblk),
            pl.BlockSpec((1, 1, 2 * D_FF), wblk),
            pl.BlockSpec((1, D_FF, D_MODEL), wblk),
            pl.BlockSpec((1, 1, D_MODEL), wblk),
        ],
        out_specs=pl.BlockSpec((EXP_BLK, HALF), lambda i, be, nbr: (i, 0)),
        scratch_shapes=[pltpu.VMEM((D_MODEL, 2 * D_FF), BF16), pltpu.VMEM((D_FF, D_MODEL), BF16)],
    )
    return pl.pallas_call(
        _expert_kernel,
        grid_spec=grid_spec,
        out_shape=jax.ShapeDtypeStruct((p_rows, HALF), jnp.uint32),
        compiler_params=pltpu.CompilerParams(
            dimension_semantics=("arbitrary",), vmem_limit_bytes=VMEM_LIMIT),
        name="experts",
    )(block_e, n_blocks, xs, wgu, bgu, wd, bd)


def _combine_kernel(dest_ref, ys_ref, x1_ref, gate_ref, g_ref, o_ref, buf, sem, *, final):
    tm = COMB_TM

    def issue(t, carry):
        for k in range(TOP_K):
            d = dest_ref[k, t]
            pltpu.make_async_copy(ys_ref.at[pl.ds(d, 1)], buf.at[k, pl.ds(t, 1)], sem).start()
        return carry

    lax.fori_loop(0, tm, issue, 0, unroll=ISSUE_UNROLL)
    for k in range(TOP_K):
        pltpu.make_async_copy(ys_ref.at[pl.ds(0, tm)], buf.at[k], sem).wait()

    gates = gate_ref[...]
    acc_lo = jnp.zeros((tm, HALF), F32)
    acc_hi = jnp.zeros((tm, HALF), F32)
    for k in range(TOP_K):
        lo, hi = _unpack_halves(buf[k])
        gk = gates[:, k:k + 1]
        acc_lo = acc_lo + gk * lo
        acc_hi = acc_hi + gk * hi
    x2 = x1_ref[...] + jnp.concatenate([acc_lo, acc_hi], axis=1)
    o_ref[...] = _rms(x2, g_ref[...]) if final else x2


def _combine(dest, ys, x1, gates_t, g, final):
    t = x1.shape[0]
    tm = COMB_TM
    return pl.pallas_call(
        functools.partial(_combine_kernel, final=final),
        grid=(t // tm,),
        in_specs=[
            pl.BlockSpec((TOP_K, tm), lambda i: (0, i), memory_space=pltpu.SMEM),
            pl.BlockSpec(memory_space=pl.ANY),
            pl.BlockSpec((tm, D_MODEL), lambda i: (i, 0)),
            pl.BlockSpec((tm, TOP_K), lambda i: (i, 0)),
            pl.BlockSpec((1, D_MODEL), lambda i: (0, 0)),
        ],
        out_specs=pl.BlockSpec((tm, D_MODEL), lambda i: (i, 0)),
        out_shape=jax.ShapeDtypeStruct((t, D_MODEL), F32),
        scratch_shapes=[pltpu.VMEM((TOP_K, tm, HALF), jnp.uint32), pltpu.SemaphoreType.DMA],
        compiler_params=pltpu.CompilerParams(
            dimension_semantics=("arbitrary",), disable_bounds_checks=True),
        name="combine",
    )(dest, ys, x1, gates_t, g)


def _t5_buckets():
    qi = jnp.arange(Q_BLOCK)
    kj = jnp.arange(KV_SPAN) - Q_BLOCK
    rel = kj[None, :] - qi[:, None]
    nb = N_BUCKETS // 2
    max_exact = nb // 2
    bucket = jnp.where(rel > 0, nb, 0)
    n = jnp.abs(rel)
    nf = jnp.maximum(n, 1).astype(jnp.float32)
    large = max_exact + (jnp.log(nf / max_exact) / math.log(MAX_DISTANCE / max_exact)
                         * (nb - max_exact)).astype(jnp.int32)
    large = jnp.minimum(large, nb - 1)
    return (bucket + jnp.where(n < max_exact, n, large)).astype(jnp.int32)


def _dup_heads(w):
    parts = []
    for j in range(SW_KV_HEADS):
        hj = w[..., j * HEAD_DIM:(j + 1) * HEAD_DIM]
        parts += [hj, hj]
    return jnp.concatenate(parts, axis=-1)


def kernel(x, attn_norm, w_in, b_in, sb_out_norm, sw_out_norm, sinks, rel_bias_table, w_out, b_out,
           ffn_norm, w_router, b_router, w_gate_up, b_gate_up, w_down, b_down, final_norm):
    bsz, seq, _ = x.shape
    t = bsz * seq
    depth = w_in.shape[0]
    scale = HEAD_DIM ** -0.5
    buckets = _t5_buckets()
    table = rel_bias_table.astype(F32)
    n_rows = t * TOP_K + N_EXPERTS * EXP_BLK
    n_blk = n_rows // EXP_BLK

    x2 = x.reshape(t, D_MODEL)
    for l in range(depth):
        def cols(a):
            o1, o2, o3 = SB_WIDTH, 2 * SB_WIDTH, 3 * SB_WIDTH
            o4 = o3 + SW_WIDTH
            o5 = o4 + SW_KV_WIDTH
            return jnp.concatenate([
                a[..., :o1] * (scale * LOG2E), a[..., o1:o2], a[..., o2:o3], a[..., o3:o4] * scale,
                _dup_heads(a[..., o4:o5]), _dup_heads(a[..., o5:])], axis=-1)

        w_l = cols(w_in[l]).astype(BF16)
        b_l = cols(b_in[l]).reshape(1, -1).astype(F32)
        widths = (SB_WIDTH, SB_WIDTH, SB_WIDTH, SW_WIDTH, 2 * SW_KV_WIDTH, 2 * SW_KV_WIDTH)
        sbq, sbk, sbv, swq, swk, swv = _inproj(x2, attn_norm[l].reshape(1, -1), w_l, b_l, widths)

        r3 = lambda a: a.reshape(bsz, seq, a.shape[-1])
        sb = _sb_attention(r3(sbq), r3(sbk), r3(sbv), sb_out_norm[l].reshape(1, -1))
        sw = _sw_attention(table, sinks[l].astype(F32), buckets, r3(swq), r3(swk), r3(swv),
                           sw_out_norm[l].reshape(1, -1))

        wo = w_out[l].astype(BF16)
        x1, h2p, top_idx, gates, rank, counts = _post(
            sb.reshape(t, SB_WIDTH), sw.reshape(t, SW_WIDTH), x2,
            wo[:SB_WIDTH], wo[SB_WIDTH:], b_out[l].reshape(1, -1), ffn_norm[l].reshape(1, -1),
            w_router[l].T, b_router[l].reshape(-1, 1).astype(F32))

        cnt = counts[:, 0].astype(jnp.int32)
        blocks_e = (cnt + EXP_BLK - 1) // EXP_BLK
        blk_end = jnp.cumsum(blocks_e)
        pstart = (blk_end - blocks_e) * EXP_BLK
        n_used = blk_end[-1:]
        last_e = jnp.max(jnp.where(blocks_e > 0, jnp.arange(N_EXPERTS), 0))
        block_e = jnp.sum(blk_end[None, :] <= jnp.arange(n_blk)[:, None], axis=1)
        block_e = jnp.minimum(block_e, last_e).astype(jnp.int32)

        dest = _dest(pstart.astype(jnp.int32), top_idx, rank)
        xs = _dispatch(dest, h2p, jnp.zeros((n_rows, HALF), jnp.uint32))
        ys = _experts(block_e, n_used.astype(jnp.int32), xs, w_gate_up[l],
                      b_gate_up[l].reshape(N_EXPERTS, 1, -1), w_down[l], b_down[l].reshape(N_EXPERTS, 1, -1))
        x2 = _combine(dest, ys, x1, gates.T, final_norm.reshape(1, -1), final=(l + 1 == depth))
    return x2.reshape(bsz, seq, D_MODEL)
```
